```python
import functools
import jax, jax.numpy as jnp
from jax import lax
import numpy as np

D_MODEL = 4096
BATCH = 4
SEQ = 2048
DEPTH = 2
DEC_BATCH = 32
DEC_SEQ = 16
PAST_LEN = 1024

CHUNK = 64
N_PAST_CHUNKS = 8
BAND_PAST = N_PAST_CHUNKS * CHUNK
ATT_WIDTH = D_MODEL // 2
N_HEADS_A = 16
HEAD_DIM = ATT_WIDTH // N_HEADS_A
REL_CLIP = 256
POOL_WIDTH = D_MODEL - ATT_WIDTH
POOL_WINDOWS = (2, 4, 8, 16)
N_POOL_GROUPS = len(POOL_WINDOWS)
POOL_GROUP = POOL_WIDTH // N_POOL_GROUPS
POOL_PAD = max(POOL_WINDOWS) - 1
MIX_IN = 3 * ATT_WIDTH + POOL_WIDTH
N_KEYS = 128
N_EXPERTS = N_KEYS * N_KEYS
PEER_HEADS = 8
PEER_HALF = 128
PEER_KEY_DIM = 2 * PEER_HALF
PEER_TOPK = 16
PEER_BLOCK = 128
EPS = 1e-6
NEG = -1e30

kernel_name = 'hybrid_chunk_attn_pool_peer_step'


def _rmsnorm(x, g):
    xf = x.astype(jnp.float32)
    y = xf * lax.rsqrt(jnp.mean(xf * xf, axis=-1, keepdims=True) + EPS)
    return (y * g.astype(jnp.float32)).astype(x.dtype)


def _band_attention(q, kb, vb, qpos, kpos, valid, rel_bias):
    s = jnp.einsum('bnqhd,bnkhd->bnhqk', q, kb).astype(jnp.float32) * (HEAD_DIM ** -0.5)
    rel = jnp.clip(qpos[:, :, None] - kpos[:, None, :], -REL_CLIP, REL_CLIP) + REL_CLIP
    bias = jnp.take(rel_bias.astype(jnp.float32), rel, axis=1)
    s = s + jnp.transpose(bias, (1, 0, 2, 3))[None]
    s = jnp.where(valid[None, :, None, None, :], s, NEG)
    p = jax.nn.softmax(s, axis=-1).astype(vb.dtype)
    return jnp.einsum('bnhqk,bnkhd->bnqhd', p, vb)


def _attn_prompt(q, k, v, rel_bias):
    B, T = q.shape[:2]
    nc = T // CHUNK
    qc = q.reshape(B, nc, CHUNK, N_HEADS_A, HEAD_DIM)
    pad = jnp.zeros((B, BAND_PAST, N_HEADS_A, HEAD_DIM), k.dtype)
    kc = jnp.concatenate([pad, k], axis=1).reshape(B, nc + N_PAST_CHUNKS, CHUNK, N_HEADS_A, HEAD_DIM)
    vc = jnp.concatenate([pad, v], axis=1).reshape(B, nc + N_PAST_CHUNKS, CHUNK, N_HEADS_A, HEAD_DIM)
    band = jnp.arange(nc)[:, None] + jnp.arange(N_PAST_CHUNKS + 1)[None, :]
    L = (N_PAST_CHUNKS + 1) * CHUNK
    kb = kc[:, band].reshape(B, nc, L, N_HEADS_A, HEAD_DIM)
    vb = vc[:, band].reshape(B, nc, L, N_HEADS_A, HEAD_DIM)
    qpos = jnp.arange(T).reshape(nc, CHUNK)
    kpos = ((band - N_PAST_CHUNKS)[:, :, None] * CHUNK + jnp.arange(CHUNK)[None, None, :]).reshape(nc, L)
    valid = kpos >= 0
    o = _band_attention(qc, kb, vb, qpos, kpos, valid, rel_bias)
    return o.reshape(B, T, ATT_WIDTH)


def _attn_sample(q, k, v, rel_bias, past_k, past_v, pos0):
    B, T = q.shape[:2]
    W = past_k.shape[1]
    kb = jnp.concatenate([past_k, k], axis=1)[:, None]
    vb = jnp.concatenate([past_v, v], axis=1)[:, None]
    qpos = (pos0 + jnp.arange(T))[None]
    kpos = (pos0 - W + jnp.arange(W + T))[None]
    valid = jnp.ones(kpos.shape, dtype=bool)
    o = _band_attention(q[:, None], kb, vb, qpos, kpos, valid, rel_bias)
    return o.reshape(B, T, ATT_WIDTH)


def _pool_mixer(p_ext, pos0, pool_w, pool_scale):
    B = p_ext.shape[0]
    T = p_ext.shape[1] - POOL_PAD
    cs = jnp.cumsum(p_ext.astype(jnp.float32), axis=1)
    cs = jnp.concatenate([jnp.zeros_like(cs[:, :1]), cs], axis=1)
    end = cs[:, POOL_PAD + 1:]
    cur = p_ext[:, POOL_PAD:].astype(jnp.float32)
    pos = pos0 + jnp.arange(T)
    diffs = []
    for g, w in enumerate(POOL_WINDOWS):
        lo, hi = g * POOL_GROUP, (g + 1) * POOL_GROUP
        start = cs[:, POOL_PAD + 1 - w:POOL_PAD + 1 - w + T, lo:hi]
        cnt = jnp.minimum(pos + 1, w).astype(jnp.float32)[None, :, None]
        diffs.append((end[..., lo:hi] - start) / cnt - cur[..., lo:hi])
    d = jnp.stack(diffs, axis=2).astype(p_ext.dtype)
    m = jnp.einsum('btgc,gcd->btgd', d, pool_w).reshape(B, T, POOL_WIDTH)
    return m * pool_scale


def _peer(x, wq, sub_keys, u, v):
    B, T, D = x.shape
    n = B * T
    xt = x.reshape(n, D)
    q = (xt @ wq).reshape(n, PEER_HEADS, 2, PEER_HALF)
    s = jnp.einsum('nhpc,hpkc->nhpk', q, sub_keys).astype(jnp.float32)
    v1, i1 = lax.top_k(s[:, :, 0], PEER_TOPK)
    v2, i2 = lax.top_k(s[:, :, 1], PEER_TOPK)
    cand = (v1[..., :, None] + v2[..., None, :]).reshape(n, PEER_HEADS, PEER_TOPK * PEER_TOPK)
    cidx = (i1[..., :, None] * N_KEYS + i2[..., None, :]).reshape(n, PEER_HEADS, PEER_TOPK * PEER_TOPK)
    top, sel = lax.top_k(cand, PEER_TOPK)
    eidx = jnp.take_along_axis(cidx, sel, axis=-1).reshape(n, PEER_HEADS * PEER_TOPK)
    gate = jax.nn.softmax(top, axis=-1).reshape(n, PEER_HEADS * PEER_TOPK).astype(x.dtype)
    nb = -(-n // PEER_BLOCK)
    padn = nb * PEER_BLOCK - n
    xb = jnp.pad(xt, ((0, padn), (0, 0))).reshape(nb, PEER_BLOCK, D)
    eb = jnp.pad(eidx, ((0, padn), (0, 0))).reshape(nb, PEER_BLOCK, PEER_HEADS * PEER_TOPK)
    gb = jnp.pad(gate, ((0, padn), (0, 0))).reshape(nb, PEER_BLOCK, PEER_HEADS * PEER_TOPK)

    def block(args):
        xk, ek, gk = args
        ue = jnp.take(u, ek, axis=0)
        h = jnp.einsum('bkd,bd->bk', ue, xk)
        a = (gk * jax.nn.gelu(h, approximate=False)).astype(v.dtype)
        ve = jnp.take(v, ek, axis=0)
        return jnp.einsum('bk,bkd->bd', a, ve)

    out = lax.map(block, (xb, eb, gb))
    return out.reshape(nb * PEER_BLOCK, D)[:n].reshape(B, T, D)


def _layer(x, pool_hist, pos0, attn_fn, norm1_g, w_in, rel_bias, pool_w, pool_scale,
           attn_out_g, pool_out_g, w_out, norm2_g, peer_wq, peer_keys, peer_u, peer_v):
    B, T = x.shape[:2]
    h = _rmsnorm(x, norm1_g)
    z = jnp.einsum('btd,de->bte', h, w_in)
    q, k, v, p = jnp.split(z, [ATT_WIDTH, 2 * ATT_WIDTH, 3 * ATT_WIDTH], axis=-1)
    q = q.reshape(B, T, N_HEADS_A, HEAD_DIM)
    k = k.reshape(B, T, N_HEADS_A, HEAD_DIM)
    v = v.reshape(B, T, N_HEADS_A, HEAD_DIM)
    a = attn_fn(q, k, v, rel_bias)
    p_ext = jnp.concatenate([pool_hist, p], axis=1)
    m = _pool_mixer(p_ext, pos0, pool_w, pool_scale)
    mix = jnp.concatenate([_rmsnorm(a, attn_out_g), _rmsnorm(m, pool_out_g)], axis=-1)
    x = x + jnp.einsum('bte,ed->btd', mix, w_out)
    x = x + _peer(_rmsnorm(x, norm2_g), peer_wq, peer_keys, peer_u, peer_v)
    return x, k, v, p_ext[:, -POOL_PAD:]


def setup_inputs(seed: int = 0) -> dict:
    key = jax.random.key(seed)
    ks = jax.random.split(key, 20)
    att_cache = min(BAND_PAST, PAST_LEN)

    def nrm(k, shape, s):
        return s * jax.random.normal(k, shape, jnp.float32)

    return {
        'x_prompt': nrm(ks[0], (BATCH, SEQ, D_MODEL), 1.0),
        'x_sample': nrm(ks[1], (DEC_BATCH, DEC_SEQ, D_MODEL), 1.0),
        'cache_attn_k': nrm(ks[2], (DEPTH, DEC_BATCH, att_cache, N_HEADS_A, HEAD_DIM), 1.0),
        'cache_attn_v': nrm(ks[3], (DEPTH, DEC_BATCH, att_cache, N_HEADS_A, HEAD_DIM), 1.0),
        'state_pool': nrm(ks[4], (DEPTH, DEC_BATCH, POOL_PAD, POOL_WIDTH), 1.0),
        'norm1_g': 1.0 + nrm(ks[5], (DEPTH, D_MODEL), 0.02),
        'w_in': nrm(ks[6], (DEPTH, D_MODEL, MIX_IN), D_MODEL ** -0.5),
        'rel_bias': nrm(ks[7], (DEPTH, N_HEADS_A, 2 * REL_CLIP + 1), 0.5),
        'pool_w': nrm(ks[8], (DEPTH, N_POOL_GROUPS, POOL_GROUP, POOL_GROUP), POOL_GROUP ** -0.5),
        'pool_scale': 1.0 + nrm(ks[9], (DEPTH, POOL_WIDTH), 0.1),
        'attn_out_g': 1.0 + nrm(ks[10], (DEPTH, ATT_WIDTH), 0.02),
        'pool_out_g': 1.0 + nrm(ks[11], (DEPTH, POOL_WIDTH), 0.02),
        'w_out': nrm(ks[12], (DEPTH, D_MODEL, D_MODEL), D_MODEL ** -0.5),
        'norm2_g': 1.0 + nrm(ks[13], (DEPTH, D_MODEL), 0.02),
        'peer_wq': nrm(ks[14], (DEPTH, D_MODEL, PEER_HEADS * PEER_KEY_DIM), D_MODEL ** -0.5),
        'peer_keys': nrm(ks[15], (DEPTH, PEER_HEADS, 2, N_KEYS, PEER_HALF), PEER_HALF ** -0.5),
        'peer_u': nrm(ks[16], (DEPTH, N_EXPERTS, D_MODEL), D_MODEL ** -0.5),
        'peer_v': nrm(ks[17], (DEPTH, N_EXPERTS, D_MODEL), 0.5),
        'final_g': 1.0 + nrm(ks[18], (D_MODEL,), 0.02),
    }


def reference(x_prompt, x_sample, cache_attn_k, cache_attn_v, state_pool, norm1_g, w_in,
              rel_bias, pool_w, pool_scale, attn_out_g, pool_out_g, w_out, norm2_g,
              peer_wq, peer_keys, peer_u, peer_v, final_g):
    xp, xs = x_prompt, x_sample
    kp, vp, pp, ksm, vsm, psm = [], [], [], [], [], []
    zero_hist = jnp.zeros((xp.shape[0], POOL_PAD, POOL_WIDTH), xp.dtype)
    win = min(BAND_PAST, xp.shape[1])
    for l in range(DEPTH):
        w = (norm1_g[l], w_in[l], rel_bias[l], pool_w[l], pool_scale[l], attn_out_g[l],
             pool_out_g[l], w_out[l], norm2_g[l], peer_wq[l], peer_keys[l], peer_u[l], peer_v[l])
        xp, k_, v_, p_ = _layer(xp, zero_hist, 0, _attn_prompt, *w)
        kp.append(k_[:, -win:])
        vp.append(v_[:, -win:])
        pp.append(p_)
        attn_s = functools.partial(_attn_sample, past_k=cache_attn_k[l], past_v=cache_attn_v[l], pos0=PAST_LEN)
        xs, k_, v_, p_ = _layer(xs, state_pool[l], PAST_LEN, attn_s, *w)
        ksm.append(k_)
        vsm.append(v_)
        psm.append(p_)
    y_prompt = _rmsnorm(xp, final_g)
    y_sample = _rmsnorm(xs, final_g)
    return (y_prompt, y_sample, jnp.stack(kp), jnp.stack(vp), jnp.stack(pp),
            jnp.stack(ksm), jnp.stack(vsm), jnp.stack(psm))
```

```python
import functools

import jax
import jax.numpy as jnp
from jax import lax
from jax.experimental import pallas as pl
from jax.experimental.pallas import tpu as pltpu

EPS = 1e-6
NEG = -1e30
CHUNK = 64
N_PAST_CHUNKS = 8
BAND_PAST = CHUNK * N_PAST_CHUNKS
REL_CLIP = 256
POOL_WINDOWS = (2, 4, 8, 16)
POOL_PAD = max(POOL_WINDOWS) - 1
HEAD_DIM = 128
N_KEYS = 128
PEER_HALF = 128
PEER_TOPK = 16

LANES = 128
SUBLANES = 8
TOKEN_TILE = 512
ATT_Q_SUB = 2 * CHUNK
ATT_WIN = ATT_Q_SUB + BAND_PAST
EXPERT_TILE = 512
SQRT_HALF = 0.7071067811865476

F32 = jnp.float32
BF16 = jnp.bfloat16


def _vmem(nbytes):
    return pltpu.CompilerParams(vmem_limit_bytes=int(nbytes))


def _normalize_rows(x_ref, g_ref, dst_ref, col0):
    rows, k = x_ref.shape
    step = 2 * SUBLANES

    def body(r, carry):
        r0 = pl.multiple_of(r * step, step)
        x = x_ref[pl.ds(r0, step), :]
        ms = jnp.mean(x * x, axis=-1, keepdims=True)
        y = x * lax.rsqrt(ms + EPS) * g_ref[...]
        dst_ref[pl.ds(r0, step), col0:col0 + k] = y.astype(dst_ref.dtype)
        return carry

    lax.fori_loop(0, rows // step, body, 0)


def _norm_matmul_kernel(*refs, n_seg, has_res):
    xs = refs[:n_seg]
    gs = refs[n_seg:2 * n_seg]
    w_ref = refs[2 * n_seg]
    pos = 2 * n_seg + 1
    res_ref = refs[pos] if has_res else None
    pos += int(has_res)
    o_ref, hn_ref = refs[pos], refs[pos + 1]

    @pl.when(pl.program_id(1) == 0)
    def _():
        col0 = 0
        for x_ref, g_ref in zip(xs, gs):
            _normalize_rows(x_ref, g_ref, hn_ref, col0)
            col0 += x_ref.shape[1]

    acc = jnp.dot(hn_ref[...], w_ref[...], preferred_element_type=F32)
    if has_res:
        acc = res_ref[...] + acc
    o_ref[...] = acc


def _norm_matmul(segs, gains, w, res=None, *, tm=TOKEN_TILE, tn=512):
    m = segs[0].shape[0]
    k_tot, n = w.shape
    n_seg = len(segs)
    in_specs = [pl.BlockSpec((tm, s.shape[1]), lambda i, j: (i, 0)) for s in segs]
    in_specs += [pl.BlockSpec((1, s.shape[1]), lambda i, j: (0, 0)) for s in segs]
    in_specs.append(pl.BlockSpec((k_tot, tn), lambda i, j: (0, j)))
    args = list(segs) + [g.reshape(1, -1) for g in gains] + [w]
    if res is not None:
        in_specs.append(pl.BlockSpec((tm, tn), lambda i, j: (i, j)))
        args.append(res)
    vmem = 2 * tm * k_tot * 4 + tm * k_tot * 2 + 2 * k_tot * tn * 2 + 4 * tm * tn * 4 + (6 << 20)
    return pl.pallas_call(
        functools.partial(_norm_matmul_kernel, n_seg=n_seg, has_res=res is not None),
        grid=(m // tm, n // tn),
        in_specs=in_specs,
        out_specs=pl.BlockSpec((tm, tn), lambda i, j: (i, j)),
        out_shape=jax.ShapeDtypeStruct((m, n), F32),
        scratch_shapes=[pltpu.VMEM((tm, k_tot), BF16)],
        compiler_params=pltpu.CompilerParams(
            dimension_semantics=("parallel", "arbitrary"), vmem_limit_bytes=int(vmem)),
        name="norm_matmul",
    )(*args)


def _band_bias(rel_bias, n_q, n_k, banded):
    r = jnp.arange(n_q)[:, None]
    j = jnp.arange(n_k)[None, :]
    rel = jnp.clip(BAND_PAST + r - j, -REL_CLIP, REL_CLIP) + REL_CLIP
    bias = jnp.take(rel_bias.astype(F32), rel, axis=1)
    if banded:
        cq = r // CHUNK
        ck = j // CHUNK
        ok = (ck >= cq) & (ck <= cq + N_PAST_CHUNKS)
        bias = jnp.where(ok[None], bias, NEG)
    return bias


def _attn_prompt_kernel(q_ref, kp_ref, kc_ref, vp_ref, vc_ref, b_ref, o_ref, kbuf, vbuf):
    qb = pl.program_id(2)
    tq = q_ref.shape[0]
    kbuf[0:tq, :] = kp_ref[...].astype(BF16)
    kbuf[tq:2 * tq, :] = kc_ref[...].astype(BF16)
    vbuf[0:tq, :] = vp_ref[...].astype(BF16)
    vbuf[tq:2 * tq, :] = vc_ref[...].astype(BF16)
    scale = HEAD_DIM ** -0.5
    has_prev = qb > 0
    for sb in range(tq // ATT_Q_SUB):
        r0 = sb * ATT_Q_SUB
        q = q_ref[r0:r0 + ATT_Q_SUB, :].astype(BF16)
        kw = kbuf[r0:r0 + ATT_WIN, :]
        s = lax.dot_general(q, kw, (((1,), (1,)), ((), ())), preferred_element_type=F32)
        s = s * scale + b_ref[0]
        n_prev = tq - r0
        if n_prev >= ATT_WIN:
            s = jnp.where(has_prev, s, NEG)
        else:
            s = jnp.concatenate(
                [jnp.where(has_prev, s[:, :n_prev], NEG), s[:, n_prev:]], axis=1)
        m = jnp.max(s, axis=-1, keepdims=True)
        p = jnp.exp(s - m)
        l = jnp.sum(p, axis=-1, keepdims=True)
        o = jnp.dot(p.astype(BF16), vbuf[r0:r0 + ATT_WIN, :], preferred_element_type=F32)
        o_ref[r0:r0 + ATT_Q_SUB, :] = o / l


def _attn_prompt(z, bias, n_batch, seq, n_heads, att_width, m_total):
    tq = TOKEN_TILE
    nqb = seq // tq
    hk = att_width // HEAD_DIM

    def cur(off):
        return pl.BlockSpec((tq, HEAD_DIM), lambda h, b, q: (b * nqb + q, off + h))

    def prev(off):
        return pl.BlockSpec((tq, HEAD_DIM),
                            lambda h, b, q: (b * nqb + jnp.maximum(q - 1, 0), off + h))

    return pl.pallas_call(
        _attn_prompt_kernel,
        grid=(n_heads, n_batch, nqb),
        in_specs=[cur(0), prev(hk), cur(hk), prev(2 * hk), cur(2 * hk),
                  pl.BlockSpec((1, ATT_Q_SUB, ATT_WIN), lambda h, b, q: (h, 0, 0))],
        out_specs=pl.BlockSpec((tq, HEAD_DIM), lambda h, b, q: (b * nqb + q, h)),
        out_shape=jax.ShapeDtypeStruct((m_total, att_width), F32),
        scratch_shapes=[pltpu.VMEM((2 * tq, HEAD_DIM), BF16), pltpu.VMEM((2 * tq, HEAD_DIM), BF16)],
        compiler_params=pltpu.CompilerParams(
            dimension_semantics=("parallel", "parallel", "arbitrary")),
        name="attn_prompt",
    )(z, z, z, z, z, bias)


def _attn_sample_kernel(q_ref, k_ref, v_ref, kc_ref, vc_ref, b_ref, abuf_ref, o_ref, *, n_heads):
    del abuf_ref
    scale = HEAD_DIM ** -0.5
    w = kc_ref.shape[0]
    for h in range(n_heads):
        c0 = h * HEAD_DIM
        q = q_ref[:, c0:c0 + HEAD_DIM].astype(BF16)
        kn = k_ref[:, c0:c0 + HEAD_DIM].astype(BF16)
        vn = v_ref[:, c0:c0 + HEAD_DIM].astype(BF16)
        kc = kc_ref[:, c0:c0 + HEAD_DIM].astype(BF16)
        vc = vc_ref[:, c0:c0 + HEAD_DIM].astype(BF16)
        nt = (((1,), (1,)), ((), ()))
        b = b_ref[h]
        s1 = lax.dot_general(q, kc, nt, preferred_element_type=F32) * scale + b[:, :w]
        s2 = lax.dot_general(q, kn, nt, preferred_element_type=F32) * scale + b[:, w:]
        m = jnp.maximum(jnp.max(s1, axis=-1, keepdims=True), jnp.max(s2, axis=-1, keepdims=True))
        p1 = jnp.exp(s1 - m)
        p2 = jnp.exp(s2 - m)
        l = jnp.sum(p1, axis=-1, keepdims=True) + jnp.sum(p2, axis=-1, keepdims=True)
        o = (jnp.dot(p1.astype(BF16), vc, preferred_element_type=F32)
             + jnp.dot(p2.astype(BF16), vn, preferred_element_type=F32))
        o_ref[:, c0:c0 + HEAD_DIM] = o / l


def _attn_sample(z, cache_k, cache_v, bias, a_buf, row0, n_streams, t_new, n_heads, att_width):
    w = cache_k.shape[1]
    rb0 = row0 // t_new

    def zspec(col):
        return pl.BlockSpec((t_new, att_width), lambda b: (rb0 + b, col))

    cspec = pl.BlockSpec((None, w, att_width), lambda b: (b, 0, 0))
    return pl.pallas_call(
        functools.partial(_attn_sample_kernel, n_heads=n_heads),
        grid=(n_streams,),
        in_specs=[zspec(0), zspec(1), zspec(2), cspec, cspec,
                  pl.BlockSpec((n_heads, t_new, w + t_new), lambda b: (0, 0, 0)),
                  pl.BlockSpec(memory_space=pl.ANY)],
        out_specs=pl.BlockSpec((t_new, att_width), lambda b: (rb0 + b, 0)),
        out_shape=jax.ShapeDtypeStruct(a_buf.shape, F32),
        input_output_aliases={6: 0},
        compiler_params=pltpu.CompilerParams(
            dimension_semantics=("parallel",), vmem_limit_bytes=40 << 20),
        name="attn_sample",
    )(z, z, z, cache_k, cache_v, bias, a_buf)


def _window_sum(ext_ref, w, n_rows, lo, hi):
    base = POOL_PAD + 1
    acc = ext_ref[base:base + n_rows, lo:hi]
    for k in range(1, w):
        acc = acc + ext_ref[base - k:base - k + n_rows, lo:hi]
    return acc


def _pool_prompt_kernel(cur_ref, halo_ref, w_ref, sc_ref, o_ref, ext_ref):
    t = pl.program_id(1)
    n_rows = cur_ref.shape[0]
    base = POOL_PAD + 1
    pg = w_ref.shape[1]
    ext_ref[0:base, :] = jnp.where(t > 0, halo_ref[...], 0.0)
    ext_ref[base:base + n_rows, :] = cur_ref[...]
    pos = t * n_rows + lax.broadcasted_iota(jnp.int32, (n_rows, 1), 0)
    for g, w in enumerate(POOL_WINDOWS):
        lo, hi = g * pg, (g + 1) * pg
        cnt = jnp.minimum(pos + 1, w).astype(F32)
        d = _window_sum(ext_ref, w, n_rows, lo, hi) / cnt - cur_ref[:, lo:hi]
        mg = jnp.dot(d.astype(BF16), w_ref[g], preferred_element_type=F32)
        o_ref[:, lo:hi] = mg * sc_ref[:, lo:hi]


def _pool_prompt(z, pool_w, pool_scale, n_batch, seq, pool_width, m_total):
    tr = TOKEN_TILE
    ntile = seq // tr
    col = (z.shape[1] - pool_width) // pool_width
    base = POOL_PAD + 1
    hb = tr // base
    ng, pg = pool_w.shape[0], pool_w.shape[1]
    return pl.pallas_call(
        _pool_prompt_kernel,
        grid=(n_batch, ntile),
        in_specs=[
            pl.BlockSpec((tr, pool_width), lambda b, t: (b * ntile + t, col)),
            pl.BlockSpec((base, pool_width),
                         lambda b, t: (jnp.maximum((b * ntile + t) * hb - 1, 0), col)),
            pl.BlockSpec((ng, pg, pg), lambda b, t: (0, 0, 0)),
            pl.BlockSpec((1, pool_width), lambda b, t: (0, 0)),
        ],
        out_specs=pl.BlockSpec((tr, pool_width), lambda b, t: (b * ntile + t, 0)),
        out_shape=jax.ShapeDtypeStruct((m_total, pool_width), F32),
        scratch_shapes=[pltpu.VMEM((base + tr, pool_width), F32)],
        compiler_params=pltpu.CompilerParams(
            dimension_semantics=("parallel", "arbitrary"), vmem_limit_bytes=40 << 20),
        name="pool_prompt",
    )(z, z, pool_w, pool_scale.reshape(1, -1))


def _pool_sample_kernel(cur_ref, hist_ref, w_ref, sc_ref, mbuf_ref, o_ref, ext_ref):
    del mbuf_ref
    ns, t_new, _ = cur_ref.shape
    base = POOL_PAD + 1
    grp = base + t_new
    pg = w_ref.shape[1]
    for s in range(ns):
        ext_ref[s * grp:s * grp + 1, :] = jnp.zeros((1, ext_ref.shape[1]), F32)
        ext_ref[s * grp + 1:s * grp + base, :] = hist_ref[s]
        ext_ref[s * grp + base:(s + 1) * grp, :] = cur_ref[s]
    n_rows = ns * grp - base
    for g, w in enumerate(POOL_WINDOWS):
        lo, hi = g * pg, (g + 1) * pg
        ws = _window_sum(ext_ref, w, n_rows, lo, hi)
        ws = jnp.concatenate([ws[s * grp:s * grp + t_new] for s in range(ns)], axis=0)
        cur = jnp.concatenate([cur_ref[s][:, lo:hi] for s in range(ns)], axis=0)
        d = ws / float(w) - cur
        mg = jnp.dot(d.astype(BF16), w_ref[g], preferred_element_type=F32)
        o_ref[:, lo:hi] = mg * sc_ref[:, lo:hi]


def _pool_sample(zs3, hist, pool_w, pool_scale, m_buf, row0, pool_width):
    n_streams, t_new, mix_in = zs3.shape
    ns = 8
    col = (mix_in - pool_width) // pool_width
    ng, pg = pool_w.shape[0], pool_w.shape[1]
    rows = ns * t_new
    return pl.pallas_call(
        _pool_sample_kernel,
        grid=(n_streams // ns,),
        in_specs=[
            pl.BlockSpec((ns, t_new, pool_width), lambda s: (s, 0, col)),
            pl.BlockSpec((ns, POOL_PAD, pool_width), lambda s: (s, 0, 0)),
            pl.BlockSpec((ng, pg, pg), lambda s: (0, 0, 0)),
            pl.BlockSpec((1, pool_width), lambda s: (0, 0)),
            pl.BlockSpec(memory_space=pl.ANY),
        ],
        out_specs=pl.BlockSpec((rows, pool_width), lambda s: (row0 // rows + s, 0)),
        out_shape=jax.ShapeDtypeStruct(m_buf.shape, F32),
        scratch_shapes=[pltpu.VMEM((ns * (POOL_PAD + 1 + t_new), pool_width), F32)],
        input_output_aliases={4: 0},
        compiler_params=pltpu.CompilerParams(dimension_semantics=("parallel",)),
        name="pool_sample",
    )(zs3, hist, pool_w, pool_scale.reshape(1, -1), m_buf)


def _peer_scores_kernel(x_ref, g_ref, wq_ref, keys_ref, st_ref, xn_ref):
    @pl.when(pl.program_id(1) == 0)
    def _():
        _normalize_rows(x_ref, g_ref, xn_ref, 0)

    q = jnp.dot(xn_ref[...], wq_ref[...], preferred_element_type=F32)
    nt = (((1,), (1,)), ((), ()))
    for p in range(2):
        qp = q[:, p * PEER_HALF:(p + 1) * PEER_HALF]
        st_ref[p] = lax.dot_general(keys_ref[0, p], qp, nt, preferred_element_type=F32,
                                    precision=lax.Precision.HIGHEST)


def _peer_scores(x, g, wq, keys):
    m, d = x.shape
    n_heads = keys.shape[0]
    tm = TOKEN_TILE
    kd = 2 * PEER_HALF
    vmem = 2 * tm * d * 4 + 2 * tm * d * 2 + 2 * d * kd * 2 + (8 << 20)
    return pl.pallas_call(
        _peer_scores_kernel,
        grid=(m // tm, n_heads),
        in_specs=[
            pl.BlockSpec((tm, d), lambda i, h: (i, 0)),
            pl.BlockSpec((1, d), lambda i, h: (0, 0)),
            pl.BlockSpec((d, kd), lambda i, h: (0, h)),
            pl.BlockSpec((1, 2, N_KEYS, PEER_HALF), lambda i, h: (h, 0, 0, 0)),
        ],
        out_specs=[
            pl.BlockSpec((2, N_KEYS, tm), lambda i, h: (h, 0, i)),
            pl.BlockSpec((tm, d), lambda i, h: (i, 0)),
        ],
        out_shape=[
            jax.ShapeDtypeStruct((2 * n_heads, N_KEYS, m), F32),
            jax.ShapeDtypeStruct((m, d), BF16),
        ],
        compiler_params=pltpu.CompilerParams(
            dimension_semantics=("parallel", "arbitrary"), vmem_limit_bytes=int(vmem)),
        name="peer_scores",
    )(x, g.reshape(1, -1), wq, keys)


def _take_top(work, n):
    out = []
    for _ in range(n):
        m = jnp.max(work, axis=0, keepdims=True)
        out.append(m)
        work = jnp.where(work == m, -jnp.inf, work)
    return out


def _peer_select_kernel(st_ref, thr_ref, e1_ref, e2_ref, cand_ref, *, n_heads):
    tn = st_ref.shape[2]

    def head(h, carry):
        for lb in range(tn // LANES):
            ls = slice(lb * LANES, (lb + 1) * LANES)
            s1 = st_ref[2 * h, :, ls]
            s2 = st_ref[2 * h + 1, :, ls]
            v1 = _take_top(s1, PEER_TOPK)
            v2 = jnp.concatenate(_take_top(s2, PEER_TOPK), axis=0)
            for a in range(PEER_TOPK):
                cand_ref[a * PEER_TOPK:(a + 1) * PEER_TOPK, :] = v1[a] + v2
            top = _take_top(cand_ref[...], PEER_TOPK + 1)
            thr = 0.5 * (top[PEER_TOPK - 1] + top[PEER_TOPK])
            z = jnp.exp(top[0] - top[0])
            for k in range(1, PEER_TOPK):
                z = z + jnp.exp(top[k] - top[0])
            thr_ref[h, :, ls] = thr - s1
            e1_ref[h, :, ls] = jnp.exp(s1 - v1[0])
            e2_ref[h, :, ls] = jnp.exp(s2 - v2[0:1]) / z
        return carry

    lax.fori_loop(0, n_heads, head, 0)


def _peer_select(st):
    two_h, nk, m = st.shape
    n_heads = two_h // 2
    tn = 256
    tab = jax.ShapeDtypeStruct((n_heads, nk, m), F32)
    spec = pl.BlockSpec((n_heads, nk, tn), lambda i: (0, 0, i))
    return pl.pallas_call(
        functools.partial(_peer_select_kernel, n_heads=n_heads),
        grid=(m // tn,),
        in_specs=[pl.BlockSpec((two_h, nk, tn), lambda i: (0, 0, i))],
        out_specs=[spec, spec, spec],
        out_shape=[tab, tab, tab],
        scratch_shapes=[pltpu.VMEM((PEER_TOPK * PEER_TOPK, LANES), F32)],
        compiler_params=pltpu.CompilerParams(dimension_semantics=("parallel",)),
        name="peer_select",
    )(st)


PEER_ROWS = 32


def _peer_dense_kernel(xn_ref, res_ref, u_ref, v_ref, s2_ref, thr_ref, e1_ref, e2_ref,
                       o_ref, ht_ref, at_ref, *, n_heads):
    e = pl.program_id(1)
    te = u_ref.shape[0]
    tn = xn_ref.shape[0]
    n_first = te // N_KEYS

    @pl.when(e == 0)
    def _():
        o_ref[...] = res_ref[...]

    nt = (((1,), (1,)), ((), ()))
    ht_ref[...] = lax.dot_general(u_ref[...], xn_ref[...], nt, preferred_element_type=F32)

    for lb in range(tn // LANES):
        ls = slice(lb * LANES, (lb + 1) * LANES)
        for jb in range(N_KEYS // PEER_ROWS):
            js = slice(jb * PEER_ROWS, (jb + 1) * PEER_ROWS)
            gate = [jnp.zeros((PEER_ROWS, LANES), F32) for _ in range(n_first)]
            for h in range(n_heads):
                s2 = s2_ref[h, js, ls]
                e2 = e2_ref[h, js, ls]
                for ii in range(n_first):
                    thr = thr_ref[h, ii:ii + 1, ls]
                    e1 = e1_ref[h, ii:ii + 1, ls]
                    gate[ii] = gate[ii] + jnp.where(s2 >= thr, e2, 0.0) * e1
            for ii in range(n_first):
                rs = slice(ii * N_KEYS + jb * PEER_ROWS, ii * N_KEYS + (jb + 1) * PEER_ROWS)
                hv = ht_ref[rs, ls]
                act = 0.5 * hv * (1.0 + lax.erf(hv * SQRT_HALF))
                at_ref[rs, ls] = (act * gate[ii]).astype(BF16)

    tn_dims = (((0,), (0,)), ((), ()))
    o_ref[...] += lax.dot_general(at_ref[...], v_ref[...], tn_dims, preferred_element_type=F32)


def _peer_dense(xn, res, u, v, st, thr, e1, e2):
    m, d = xn.shape
    n_exp = u.shape[0]
    n_heads = thr.shape[0]
    tn, te = TOKEN_TILE, EXPERT_TILE
    one = pl.Buffered(1)
    n_first = te // N_KEYS
    st4 = st.reshape(n_heads, 2, N_KEYS, m)
    thr4 = thr.reshape(n_heads, N_KEYS // n_first, n_first, m)
    e14 = e1.reshape(n_heads, N_KEYS // n_first, n_first, m)
    tab = pl.BlockSpec((n_heads, N_KEYS, tn), lambda i, e: (0, 0, i), pipeline_mode=one)
    first = pl.BlockSpec((n_heads, None, n_first, tn), lambda i, e: (0, e, 0, i))
    vmem = (tn * d * 2 + tn * d * 4 + tn * d * 4 + 4 * te * d * 2
            + 4 * n_heads * N_KEYS * tn * 4 + te * tn * 6 + (6 << 20))
    return pl.pallas_call(
        functools.partial(_peer_dense_kernel, n_heads=n_heads),
        grid=(m // tn, n_exp // te),
        in_specs=[
            pl.BlockSpec((tn, d), lambda i, e: (i, 0), pipeline_mode=one),
            pl.BlockSpec((tn, d), lambda i, e: (i, 0), pipeline_mode=one),
            pl.BlockSpec((te, d), lambda i, e: (e, 0)),
            pl.BlockSpec((te, d), lambda i, e: (e, 0)),
            pl.BlockSpec((n_heads, None, N_KEYS, tn), lambda i, e: (0, 1, 0, i), pipeline_mode=one),
            first, first, tab,
        ],
        out_specs=pl.BlockSpec((tn, d), lambda i, e: (i, 0), pipeline_mode=one),
        out_shape=jax.ShapeDtypeStruct((m, d), F32),
        scratch_shapes=[pltpu.VMEM((te, tn), F32), pltpu.VMEM((te, tn), BF16)],
        compiler_params=pltpu.CompilerParams(
            dimension_semantics=("parallel", "arbitrary"), vmem_limit_bytes=int(vmem)),
        name="peer_dense",
    )(xn, res, u, v, st4, thr4, e14, e2)


def _final_norm_kernel(x_ref, g_ref, o_ref):
    _normalize_rows(x_ref, g_ref, o_ref, 0)


def _final_norm(x, g):
    m, d = x.shape
    tm = TOKEN_TILE
    return pl.pallas_call(
        _final_norm_kernel,
        grid=(m // tm,),
        in_specs=[pl.BlockSpec((tm, d), lambda i: (i, 0)), pl.BlockSpec((1, d), lambda i: (0, 0))],
        out_specs=pl.BlockSpec((tm, d), lambda i: (i, 0)),
        out_shape=jax.ShapeDtypeStruct((m, d), F32),
        compiler_params=pltpu.CompilerParams(
            dimension_semantics=("parallel",), vmem_limit_bytes=40 << 20),
        name="final_norm",
    )(x, g.reshape(1, -1))


def kernel(x_prompt, x_sample, cache_attn_k, cache_attn_v, state_pool, norm1_g, w_in, rel_bias,
           pool_w, pool_scale, attn_out_g, pool_out_g, w_out, norm2_g, peer_wq, peer_keys,
           peer_u, peer_v, final_g):
    n_batch, seq, d = x_prompt.shape
    n_streams, t_new, _ = x_sample.shape
    depth, _, w_cache, n_heads, head_dim = cache_attn_k.shape
    att_width = n_heads * head_dim
    pool_width = state_pool.shape[-1]
    peer_heads = peer_keys.shape[1]
    mp = n_batch * seq
    ms = n_streams * t_new
    m = mp + ms
    win = min(BAND_PAST, seq)
    assert head_dim == HEAD_DIM and w_cache == BAND_PAST and t_new == POOL_PAD + 1
    assert seq % TOKEN_TILE == 0 and m % TOKEN_TILE == 0 and mp % (8 * t_new) == 0
    assert peer_keys.shape[3] == N_KEYS and peer_keys.shape[4] == PEER_HALF

    x = jnp.concatenate([x_prompt.reshape(mp, d), x_sample.reshape(ms, d)], axis=0)
    outs = {k: [] for k in ("kp", "vp", "pp", "ks", "vs", "ps")}
    for l in range(depth):
        z = _norm_matmul([x], [norm1_g[l]], w_in[l].astype(BF16))
        zp = z[:mp].reshape(n_batch, seq, -1)
        zs = z[mp:].reshape(n_streams, t_new, -1)
        for name, src, c0 in (("kp", zp[:, seq - win:], att_width), ("vp", zp[:, seq - win:], 2 * att_width),
                              ("ks", zs, att_width), ("vs", zs, 2 * att_width)):
            outs[name].append(src[..., c0:c0 + att_width].reshape(*src.shape[:2], n_heads, head_dim))
        outs["pp"].append(zp[:, seq - POOL_PAD:, 3 * att_width:])
        outs["ps"].append(zs[:, t_new - POOL_PAD:, 3 * att_width:])

        bias_p = _band_bias(rel_bias[l], ATT_Q_SUB, ATT_WIN, banded=True)
        bias_s = _band_bias(rel_bias[l], t_new, w_cache + t_new, banded=False)
        a = _attn_prompt(z, bias_p, n_batch, seq, n_heads, att_width, m)
        a = _attn_sample(z, cache_attn_k[l].reshape(n_streams, w_cache, att_width),
                         cache_attn_v[l].reshape(n_streams, w_cache, att_width),
                         bias_s, a, mp, n_streams, t_new, n_heads, att_width)
        pw = pool_w[l].astype(BF16)
        mix = _pool_prompt(z, pw, pool_scale[l], n_batch, seq, pool_width, m)
        mix = _pool_sample(zs, state_pool[l], pw, pool_scale[l], mix, mp, pool_width)

        x = _norm_matmul([a, mix], [attn_out_g[l], pool_out_g[l]], w_out[l].astype(BF16), res=x)

        st, xn = _peer_scores(x, norm2_g[l], peer_wq[l].astype(BF16), peer_keys[l])
        thr, e1, e2 = _peer_select(st)
        x = _peer_dense(xn, x, peer_u[l].astype(BF16), peer_v[l].astype(BF16), st, thr, e1, e2)

    y = _final_norm(x, final_g)
    return (y[:mp].reshape(n_batch, seq, d), y[mp:].reshape(n_streams, t_new, d),
            jnp.stack(outs["kp"]), jnp.stack(outs["vp"]), jnp.stack(outs["pp"]),
            jnp.stack(outs["ks"]), jnp.stack(outs["vs"]), jnp.stack(outs["ps"]))
```
